```python
import jax, jax.numpy as jnp
from jax import lax
import numpy as np

D_MODEL = 1024
BATCH = 8
SEQ = 4096
DEPTH = 1

CTX_LEN = 256
GRID_W = 64
D_SSM = D_MODEL
SSM_HEADDIM = 64
SSM_HEADS = D_SSM // SSM_HEADDIM
SSM_GROUPS = 2
D_STATE = 128
D_CONV = 5
CHUNK = 128
D_POOL = D_MODEL
POOL_WINDOWS = (2, 4, 8, 16)
POOL_GROUP = D_POOL // len(POOL_WINDOWS)
D_MIX = D_SSM + D_POOL
D_XBC = D_SSM + 2 * SSM_GROUPS * D_STATE
D_IN_PROJ = D_SSM + D_XBC + 2 * SSM_HEADS + D_POOL
D_FF = 4 * D_MODEL
EPS = 1e-6
IN_SPLITS = (D_SSM, D_SSM + D_XBC, D_SSM + D_XBC + 2 * SSM_HEADS)

kernel_name = "hybrid_ssd_pool_dit_block"


def rmsnorm(x, g):
    xf = x.astype(jnp.float32)
    return xf * lax.rsqrt(jnp.mean(xf * xf, axis=-1, keepdims=True) + EPS) * g.astype(jnp.float32)


def modulate(h, shift, scale):
    return h * (1.0 + scale) + shift


def dwconv_centred(u, w, b):
    c = u.shape[-1]
    k = w.shape[0]
    out = lax.conv_general_dilated(
        u, w.astype(jnp.float32)[:, None, :], window_strides=(1,),
        padding=((k // 2, k // 2),), dimension_numbers=('NWC', 'WIO', 'NWC'),
        feature_group_count=c)
    return out + b.astype(jnp.float32)


def box_mean(u, w, axis):
    L = u.shape[axis]
    cs = jnp.cumsum(u, axis=axis)
    cs = jnp.concatenate([jnp.zeros_like(lax.slice_in_dim(cs, 0, 1, axis=axis)), cs], axis=axis)
    t = jnp.arange(L)
    lo = jnp.clip(t - w // 2, 0, L)
    hi = jnp.clip(t + w // 2, 0, L)
    s = jnp.take(cs, hi, axis=axis) - jnp.take(cs, lo, axis=axis)
    cnt = (hi - lo).astype(u.dtype).reshape((L,) + (1,) * (u.ndim - axis - 1))
    return s / cnt


def pool_mixer(u, pool_w, pool_scale, grid):
    b, l, _ = u.shape
    ug = u.reshape(b, l, len(POOL_WINDOWS), POOL_GROUP)
    outs = []
    for gi, w in enumerate(POOL_WINDOWS):
        ui = ug[:, :, gi]
        if grid:
            rows = l // GRID_W
            v = ui.reshape(b, rows, GRID_W, POOL_GROUP)
            m = box_mean(box_mean(v, w, 1), w, 2).reshape(b, l, POOL_GROUP)
        else:
            m = box_mean(ui, w, 1)
        outs.append(m - ui)
    d = jnp.stack(outs, axis=2)
    y = jnp.einsum('blgc,gcd->blgd', d, pool_w.astype(jnp.float32)).reshape(b, l, D_POOL)
    return y * pool_scale.astype(jnp.float32)


def ssd_chunked(xs, dt, a, bm, cm, h0):
    b, l, h, p = xs.shape
    g, n = bm.shape[2], bm.shape[3]
    r = h // g
    nc = l // CHUNK
    xdt = (xs * dt[..., None]).reshape(b, nc, CHUNK, g, r, p)
    acs = jnp.cumsum((dt * a).reshape(b, nc, CHUNK, g, r), axis=2)
    bc = bm.reshape(b, nc, CHUNK, g, n)
    cc = cm.reshape(b, nc, CHUNK, g, n)
    lower = jnp.tril(jnp.ones((CHUNK, CHUNK), dtype=bool))[:, :, None, None]
    seg = acs[:, :, :, None] - acs[:, :, None, :]
    decay = jnp.exp(jnp.where(lower, seg, -jnp.inf))
    cb = jnp.einsum('bctgn,bcsgn->bctsg', cc, bc)
    y_diag = jnp.einsum('bctsgr,bcsgrp->bctgrp', cb[..., None] * decay, xdt)
    st = jnp.einsum('bcsgn,bcsgr,bcsgrp->bcgrpn', bc, jnp.exp(acs[:, :, -1:] - acs), xdt)

    def step(s, inp):
        dec, add = inp
        return s * dec[..., None, None] + add, s

    s_final, s_in = lax.scan(step, h0, (jnp.moveaxis(jnp.exp(acs[:, :, -1]), 1, 0),
                                        jnp.moveaxis(st, 1, 0)))
    s_in = jnp.moveaxis(s_in, 0, 1)
    y_off = jnp.einsum('bctgn,bcgrpn,bctgr->bctgrp', cc, s_in, jnp.exp(acs))
    return (y_diag + y_off).reshape(b, l, h, p), s_final


def ssm_project(h, w_in, conv_w, conv_b, dt_bias):
    b, l, _ = h.shape
    proj = h @ w_in.astype(jnp.float32)
    z, xbc, dt_raw, u = jnp.split(proj, IN_SPLITS, axis=-1)
    xbc = jax.nn.silu(dwconv_centred(xbc, conv_w, conv_b))
    xs, bm, cm = jnp.split(xbc, (D_SSM, D_SSM + SSM_GROUPS * D_STATE), axis=-1)
    xs = xs.reshape(b, l, SSM_HEADS, SSM_HEADDIM)
    bm = bm.reshape(b, l, SSM_GROUPS, D_STATE)
    cm = cm.reshape(b, l, SSM_GROUPS, D_STATE)
    dt = jax.nn.softplus(dt_raw.reshape(b, l, 2, SSM_HEADS) + dt_bias.astype(jnp.float32))
    return z, xs, bm, cm, dt, u


def bidir_ssd(xs, dt, a_log, bm, cm, h0_f, h0_b):
    a = -jnp.exp(a_log.astype(jnp.float32))
    rev = lambda t: jnp.flip(t, axis=1)
    y_f, s_f = ssd_chunked(xs, dt[:, :, 0], a[0], bm, cm, h0_f)
    y_b, s_b = ssd_chunked(rev(xs), rev(dt[:, :, 1]), a[1], rev(bm), rev(cm), h0_b)
    return y_f + rev(y_b), s_f, s_b


def mixer_out(y_scan, xs, z, u, d_skip, ssm_norm_w, pool_w, pool_scale, w_out, grid):
    b, l = xs.shape[:2]
    y = y_scan + d_skip.astype(jnp.float32)[:, None] * xs
    y = y.reshape(b, l, SSM_GROUPS, -1) * jax.nn.silu(z).reshape(b, l, SSM_GROUPS, -1)
    y = rmsnorm(y, ssm_norm_w.reshape(SSM_GROUPS, -1)).reshape(b, l, D_SSM)
    pm = pool_mixer(u, pool_w, pool_scale, grid)
    return jnp.concatenate([y, pm], axis=-1) @ w_out.astype(jnp.float32)


def sq_relu_mlp(h, w1, w2):
    return jnp.square(jax.nn.relu(h @ w1.astype(jnp.float32))) @ w2.astype(jnp.float32)


def setup_inputs(seed: int = 0) -> dict:
    key = jax.random.key(seed)
    ks = jax.random.split(key, 24)
    nrm = lambda k, shape, s: jax.random.normal(k, shape, jnp.float32) * s
    dt0 = jnp.exp(jax.random.uniform(ks[13], (DEPTH, 2, SSM_HEADS), jnp.float32,
                                     np.log(1e-3), np.log(1e-1)))
    return {
        "x": nrm(ks[0], (BATCH, SEQ, D_MODEL), 1.0),
        "c": nrm(ks[1], (BATCH, D_MODEL), 1.0),
        "ctx": nrm(ks[2], (BATCH, CTX_LEN, D_MODEL), 1.0),
        "c_ctx": nrm(ks[3], (D_MODEL,), 1.0),
        "w_ada": nrm(ks[4], (DEPTH, D_MODEL, 6 * D_MODEL), 0.5 * D_MODEL ** -0.5),
        "b_ada": nrm(ks[5], (DEPTH, 6 * D_MODEL), 0.02),
        "pre_mix_g": 1.0 + nrm(ks[6], (DEPTH, D_MODEL), 0.02),
        "post_mix_g": 1.0 + nrm(ks[7], (DEPTH, D_MODEL), 0.02),
        "pre_mlp_g": 1.0 + nrm(ks[8], (DEPTH, D_MODEL), 0.02),
        "post_mlp_g": 1.0 + nrm(ks[9], (DEPTH, D_MODEL), 0.02),
        "w_in": nrm(ks[10], (DEPTH, D_MODEL, D_IN_PROJ), D_MODEL ** -0.5),
        "conv_w": nrm(ks[11], (DEPTH, D_CONV, D_XBC), D_CONV ** -0.5),
        "conv_b": nrm(ks[12], (DEPTH, D_XBC), 0.02),
        "dt_bias": dt0 + jnp.log(-jnp.expm1(-dt0)),
        "a_log": jnp.log(jax.random.uniform(ks[14], (DEPTH, 2, SSM_HEADS), jnp.float32, 1.0, 16.0)),
        "d_skip": 1.0 + nrm(ks[15], (DEPTH, SSM_HEADS), 0.1),
        "ssm_norm_w": 1.0 + nrm(ks[16], (DEPTH, D_SSM), 0.02),
        "pool_w": nrm(ks[17], (DEPTH, len(POOL_WINDOWS), POOL_GROUP, POOL_GROUP), POOL_GROUP ** -0.5),
        "pool_scale": 1.0 + nrm(ks[18], (DEPTH, D_POOL), 0.1),
        "w_out": nrm(ks[19], (DEPTH, D_MIX, D_MODEL), D_MIX ** -0.5),
        "w_mlp1": nrm(ks[20], (DEPTH, D_MODEL, D_FF), D_MODEL ** -0.5),
        "w_mlp2": nrm(ks[21], (DEPTH, D_FF, D_MODEL), D_FF ** -0.5),
    }


def reference(x, c, ctx, c_ctx, w_ada, b_ada, pre_mix_g, post_mix_g, pre_mlp_g, post_mlp_g,
              w_in, conv_w, conv_b, dt_bias, a_log, d_skip, ssm_norm_w, pool_w, pool_scale,
              w_out, w_mlp1, w_mlp2):
    b = x.shape[0]
    xs_ = x.astype(jnp.float32)
    cs_ = ctx.astype(jnp.float32)
    h0 = jnp.zeros((b, SSM_GROUPS, SSM_HEADS // SSM_GROUPS, SSM_HEADDIM, D_STATE), jnp.float32)
    for i in range(DEPTH):
        wa = w_ada[i].astype(jnp.float32)
        ba = b_ada[i].astype(jnp.float32)
        ada = (jax.nn.silu(c.astype(jnp.float32)) @ wa + ba)[:, None, :]
        sh1, sc1, g1, sh2, sc2, g2 = jnp.split(ada, 6, axis=-1)
        ada_c = jax.nn.silu(c_ctx.astype(jnp.float32)) @ wa + ba
        csh1, csc1, cg1, csh2, csc2, cg2 = jnp.split(ada_c, 6, axis=-1)
        hc = modulate(rmsnorm(cs_, pre_mix_g[i]), csh1, csc1)
        zc, xc, bc, cc, dtc, uc = ssm_project(hc, w_in[i], conv_w[i], conv_b[i], dt_bias[i])
        yc, s_f, s_b = bidir_ssd(xc, dtc, a_log[i], bc, cc, h0, h0)
        hx = modulate(rmsnorm(xs_, pre_mix_g[i]), sh1, sc1)
        zx, xx, bx, cx, dtx, ux = ssm_project(hx, w_in[i], conv_w[i], conv_b[i], dt_bias[i])
        yx, _, _ = bidir_ssd(xx, dtx, a_log[i], bx, cx, s_f, s_b)
        mix_x = mixer_out(yx, xx, zx, ux, d_skip[i], ssm_norm_w[i], pool_w[i], pool_scale[i],
                          w_out[i], True)
        xs_ = xs_ + g1 * rmsnorm(mix_x, post_mix_g[i])
        hm = modulate(rmsnorm(xs_, pre_mlp_g[i]), sh2, sc2)
        xs_ = xs_ + g2 * rmsnorm(sq_relu_mlp(hm, w_mlp1[i], w_mlp2[i]), post_mlp_g[i])
        if i < DEPTH - 1:
            mix_c = mixer_out(yc, xc, zc, uc, d_skip[i], ssm_norm_w[i], pool_w[i], pool_scale[i],
                              w_out[i], False)
            cs_ = cs_ + cg1 * rmsnorm(mix_c, post_mix_g[i])
            hcm = modulate(rmsnorm(cs_, pre_mlp_g[i]), csh2, csc2)
            cs_ = cs_ + cg2 * rmsnorm(sq_relu_mlp(hcm, w_mlp1[i], w_mlp2[i]), post_mlp_g[i])
    return xs_.astype(x.dtype)
```

```python
import functools

import jax
import jax.numpy as jnp
import numpy as np
from jax import lax
from jax.experimental import pallas as pl
from jax.experimental.pallas import tpu as pltpu

F32 = jnp.float32
BF16 = jnp.bfloat16

D_MODEL = 1024
GRID_W = 64
N_HEADS = 16
HEADDIM = 64
N_GROUPS = 2
D_STATE = 128
D_CONV = 5
CHUNK = 128
D_SSM = N_HEADS * HEADDIM
D_BC = N_GROUPS * D_STATE
D_XBC = D_SSM + 2 * D_BC
POOL_WINDOWS = (2, 4, 8, 16)
POOL_GROUP = 256
D_POOL = POOL_GROUP * len(POOL_WINDOWS)
D_FF = 4 * D_MODEL
EPS = 1e-6

LANES = 128
BF16_SUBLANES = 16
VMEM_LIMIT = 56 * 1024 * 1024
HALO = BF16_SUBLANES
POOL_PAD = (max(POOL_WINDOWS) // 2) * GRID_W


def _params(*sem):
    return pltpu.CompilerParams(dimension_semantics=sem, vmem_limit_bytes=VMEM_LIMIT)


def _dot(a, b):
    return jnp.dot(a, b, preferred_element_type=F32)


def _dot_t(a, b):
    return lax.dot_general(a, b, (((0,), (0,)), ((), ())), preferred_element_type=F32)


def _split(a):
    hi = a.astype(BF16)
    lo = (a - hi.astype(F32)).astype(BF16)
    return hi, lo


def _silu(v):
    return v * (1.0 / (1.0 + jnp.exp(-v)))


def _softplus(v):
    return jnp.maximum(v, 0.0) + jnp.log(1.0 + jnp.exp(-jnp.abs(v)))


def _rms(v):
    return v * lax.rsqrt(jnp.mean(v * v, axis=-1, keepdims=True) + EPS)


def _const_spec(shape):
    nd = len(shape)
    return pl.BlockSpec(shape, lambda *_: (0,) * nd)


def _ada_kernel(c_ref, w_ref, b_ref, o_ref):
    s = _silu(c_ref[...])
    o_ref[...] = jnp.dot(s, w_ref[...], precision=lax.Precision.HIGHEST,
                         preferred_element_type=F32) + b_ref[...]


def _ada(cc, w, b):
    rows, d = cc.shape
    n = w.shape[1]
    tn = 1536
    return pl.pallas_call(
        _ada_kernel,
        out_shape=jax.ShapeDtypeStruct((rows, n), F32),
        grid=(n // tn,),
        in_specs=[_const_spec((rows, d)),
                  pl.BlockSpec((d, tn), lambda j: (0, j)),
                  pl.BlockSpec((1, tn), lambda j: (0, j))],
        out_specs=pl.BlockSpec((rows, tn), lambda j: (0, j)),
        compiler_params=_params("parallel"),
        name="ada",
    )(cc, w, b)


def _inproj_kernel(x_ref, sh_ref, sc_ref, g_ref, wx_ref, wd_ref, *rest, with_zu):
    h = _rms(x_ref[...]) * g_ref[...]
    h = h * (1.0 + sc_ref[0]) + sh_ref[0]
    hb = h.astype(BF16)
    if with_zu:
        wz_ref, wu_ref, xbc_ref, dt_ref, z_ref, u_ref = rest
        z_ref[...] = _dot(hb, wz_ref[...]).astype(BF16)
        u_ref[...] = _dot(hb, wu_ref[...]).astype(BF16)
    else:
        xbc_ref, dt_ref = rest
    xbc_ref[...] = _dot(hb, wx_ref[...]).astype(BF16)
    dt_ref[...] = _dot(hb, wd_ref[...])


def _in_proj(x2d, shift, scale, gain, wx, wd, wz=None, wu=None, *, rows_per_mod, tm):
    t, d = x2d.shape
    with_zu = wz is not None
    tiles_per_mod = rows_per_mod // tm
    mod_spec = pl.BlockSpec((1, 1, d), lambda i: (i // tiles_per_mod, 0, 0))
    row = lambda n: pl.BlockSpec((tm, n), lambda i: (i, 0))
    in_specs = [row(d), mod_spec, mod_spec, _const_spec((1, d)),
                _const_spec(wx.shape), _const_spec(wd.shape)]
    args = [x2d, shift, scale, gain, wx, wd]
    out_shape = [jax.ShapeDtypeStruct((t, D_XBC), BF16), jax.ShapeDtypeStruct((t, LANES), F32)]
    out_specs = [row(D_XBC), row(LANES)]
    if with_zu:
        in_specs += [_const_spec(wz.shape), _const_spec(wu.shape)]
        args += [wz, wu]
        out_shape += [jax.ShapeDtypeStruct((t, D_SSM), BF16), jax.ShapeDtypeStruct((t, D_POOL), BF16)]
        out_specs += [row(D_SSM), row(D_POOL)]
    return pl.pallas_call(
        functools.partial(_inproj_kernel, with_zu=with_zu),
        out_shape=out_shape,
        grid=(t // tm,),
        in_specs=in_specs,
        out_specs=out_specs,
        compiler_params=_params("parallel"),
        name="in_proj" if with_zu else "in_proj_ctx",
    )(*args)


def _conv_kernel(prev_ref, main_ref, next_ref, w_ref, b_ref, o_ref, ext_ref, *, tiles_per_seq, tc):
    i = pl.program_id(0)
    pos = i % tiles_per_seq
    keep_prev = jnp.where(pos == 0, 0.0, 1.0)
    keep_next = jnp.where(pos == tiles_per_seq - 1, 0.0, 1.0)
    ext_ref[0:HALO, :] = prev_ref[...].astype(F32) * keep_prev
    ext_ref[HALO:HALO + tc, :] = main_ref[...].astype(F32)
    ext_ref[HALO + tc:2 * HALO + tc, :] = next_ref[...].astype(F32) * keep_next
    acc = b_ref[...] + w_ref[0:1, :] * ext_ref[HALO - 2:HALO - 2 + tc, :]
    for k in range(1, D_CONV):
        acc = acc + w_ref[k:k + 1, :] * ext_ref[HALO - 2 + k:HALO - 2 + k + tc, :]
    o_ref[...] = _silu(acc).astype(BF16)


def _conv(xbc_raw, conv_w, conv_b, *, seq_len, tc):
    t = xbc_raw.shape[0]
    tiles_per_seq = seq_len // tc
    halo_blocks = tc // HALO
    last_halo = t // HALO - 1
    return pl.pallas_call(
        functools.partial(_conv_kernel, tiles_per_seq=tiles_per_seq, tc=tc),
        out_shape=jax.ShapeDtypeStruct((t, D_XBC), BF16),
        grid=(t // tc,),
        in_specs=[pl.BlockSpec((HALO, D_XBC), lambda i: (jnp.maximum(i * halo_blocks - 1, 0), 0)),
                  pl.BlockSpec((tc, D_XBC), lambda i: (i, 0)),
                  pl.BlockSpec((HALO, D_XBC), lambda i: (jnp.minimum((i + 1) * halo_blocks, last_halo), 0)),
                  _const_spec(conv_w.shape), _const_spec(conv_b.shape)],
        out_specs=pl.BlockSpec((tc, D_XBC), lambda i: (i, 0)),
        scratch_shapes=[pltpu.VMEM((tc + 2 * HALO, D_XBC), F32)],
        compiler_params=_params("parallel"),
        name="conv",
    )(xbc_raw, xbc_raw, xbc_raw, conv_w, conv_b)


def _tri(lower):
    r = lax.broadcasted_iota(jnp.int32, (CHUNK, CHUNK), 0)
    c = lax.broadcasted_iota(jnp.int32, (CHUNK, CHUNK), 1)
    m = (r >= c) if lower else (r <= c)
    return jnp.where(m, 1.0, 0.0).astype(BF16)


def _cum(tri, v):
    hi, lo = _split(v)
    return _dot(tri, hi) + _dot(tri, lo)


def _expand(v, e):
    hi, lo = _split(v)
    return _dot(hi, e) + _dot(lo, e)


def _head_expander(offset):
    r = lax.broadcasted_iota(jnp.int32, (LANES, D_SSM), 0)
    c = lax.broadcasted_iota(jnp.int32, (LANES, D_SSM), 1)
    return jnp.where(r == c // HEADDIM + offset, 1.0, 0.0).astype(BF16)


def _chunk_state(xbc, raw, bias, a, state, tri, e, edge):
    dt = _softplus(raw + bias)
    acs = _cum(tri, dt * a)
    end = acs[edge:edge + 1, :]
    w = jnp.exp(end - acs) * dt
    xw = (xbc[:, :D_SSM].astype(F32) * _expand(w, e)).astype(BF16)
    half = D_SSM // N_GROUPS
    st = jnp.concatenate(
        [_dot_t(xbc[:, D_SSM + g * D_STATE:D_SSM + (g + 1) * D_STATE], xw[:, g * half:(g + 1) * half])
         for g in range(N_GROUPS)], axis=1)
    dec = _expand(jnp.broadcast_to(jnp.exp(end), (8, LANES)), e)[0:1, :]
    return state * dec + st


def _states_kernel(xf_ref, df_ref, xb_ref, db_ref, bias_ref, alog_ref, h0f_ref, h0b_ref,
                   sfin_ref, sbin_ref, sff_ref, sbf_ref, sf_ref, sb_ref, *, nc):
    j = pl.program_id(1)

    @pl.when(j == 0)
    def _():
        sf_ref[...] = h0f_ref[0]
        sb_ref[...] = h0b_ref[0]

    a = -jnp.exp(alog_ref[...])
    bias = bias_ref[...]
    sf = sf_ref[...]
    sb = sb_ref[...]
    sfin_ref[0] = sf.astype(BF16)
    sbin_ref[0] = sb.astype(BF16)
    sf = _chunk_state(xf_ref[...], df_ref[...], bias, a, sf, _tri(True), _head_expander(0), CHUNK - 1)
    sb = _chunk_state(xb_ref[...], db_ref[...], bias, a, sb, _tri(False), _head_expander(N_HEADS), 0)
    sf_ref[...] = sf
    sb_ref[...] = sb

    @pl.when(j == nc - 1)
    def _():
        sff_ref[0] = sf
        sbf_ref[0] = sb


def _states(xbc, dt_raw, bias, alog, h0f, h0b, *, batch, nc):
    fwd = lambda n: pl.BlockSpec((CHUNK, n), lambda b, j: (b * nc + j, 0))
    bwd = lambda n: pl.BlockSpec((CHUNK, n), lambda b, j: (b * nc + nc - 1 - j, 0))
    st_b = pl.BlockSpec((1, D_STATE, D_SSM), lambda b, j: (b, 0, 0))
    st_shape = jax.ShapeDtypeStruct((batch * nc, D_STATE, D_SSM), BF16)
    fin_shape = jax.ShapeDtypeStruct((batch, D_STATE, D_SSM), F32)
    return pl.pallas_call(
        functools.partial(_states_kernel, nc=nc),
        out_shape=[st_shape, st_shape, fin_shape, fin_shape],
        grid=(batch, nc),
        in_specs=[fwd(D_XBC), fwd(LANES), bwd(D_XBC), bwd(LANES),
                  _const_spec((1, LANES)), _const_spec((1, LANES)), st_b, st_b],
        out_specs=[pl.BlockSpec((1, D_STATE, D_SSM), lambda b, j: (b * nc + j, 0, 0)),
                   pl.BlockSpec((1, D_STATE, D_SSM), lambda b, j: (b * nc + nc - 1 - j, 0, 0)),
                   st_b, st_b],
        scratch_shapes=[pltpu.VMEM((D_STATE, D_SSM), F32), pltpu.VMEM((D_STATE, D_SSM), F32)],
        compiler_params=_params("parallel", "arbitrary"),
        name="states",
    )(xbc, dt_raw, xbc, dt_raw, bias, alog, h0f, h0b)


def _block_diag_pair(v):
    lane = lax.broadcasted_iota(jnp.int32, v.shape, 1)
    zero = jnp.zeros_like(v)
    return jnp.concatenate([jnp.where(lane < HEADDIM, v, zero), jnp.where(lane >= HEADDIM, v, zero)], axis=0)


def _ssd_kernel(xbc_ref, dt_ref, z_ref, sf_ref, sb_ref, bias_ref, alog_ref, dskip_ref, nw_ref, o_ref):
    xbc = xbc_ref[...]
    dt = _softplus(dt_ref[...] + bias_ref[...])
    dta = dt * (-jnp.exp(alog_ref[...]))
    a_col = _cum(_tri(True), dta)
    r_col = _cum(_tri(False), dta)
    a_row = a_col.T
    r_row = r_col.T
    dt_row = dt.T

    t_idx = lax.broadcasted_iota(jnp.int32, (CHUNK, CHUNK), 0)
    s_idx = lax.broadcasted_iota(jnp.int32, (CHUNK, CHUNK), 1)
    below = t_idx > s_idx
    diag = t_idx == s_idx

    sf = sf_ref[0]
    sb = sb_ref[0]
    cbs, cs = [], []
    for g in range(N_GROUPS):
        b_g = xbc[:, D_SSM + g * D_STATE:D_SSM + (g + 1) * D_STATE]
        c_g = xbc[:, D_SSM + D_BC + g * D_STATE:D_SSM + D_BC + (g + 1) * D_STATE]
        cbs.append(lax.dot_general(c_g, b_g, (((1,), (1,)), ((), ())), preferred_element_type=F32))
        cs.append(c_g.astype(F32))
    ys = []
    for pair in range(N_HEADS // 2):
        g = (2 * pair) // (N_HEADS // N_GROUPS)
        cb, c_f32 = cbs[g], cs[g]
        m_parts, cf_parts, cr_parts = [], [], []
        for h in (2 * pair, 2 * pair + 1):
            hb = N_HEADS + h
            af_c = jnp.broadcast_to(a_col[:, h:h + 1], (CHUNK, CHUNK))
            rb_c = jnp.broadcast_to(r_col[:, hb:hb + 1], (CHUNK, CHUNK))
            seg = jnp.where(below, af_c - a_row[h:h + 1, :], rb_c - r_row[hb:hb + 1, :])
            dtf = dt_row[h:h + 1, :]
            dtb = dt_row[hb:hb + 1, :]
            wgt = jnp.where(below, dtf, jnp.where(diag, dtf + dtb, dtb))
            m_parts.append((cb * jnp.exp(seg) * wgt).astype(BF16))
            cf_parts.append((c_f32 * jnp.exp(af_c)).astype(BF16))
            cr_parts.append((c_f32 * jnp.exp(rb_c)).astype(BF16))
        lanes = slice(pair * LANES, (pair + 1) * LANES)
        lhs = jnp.concatenate(m_parts + cf_parts + cr_parts, axis=1)
        rhs = jnp.concatenate([_block_diag_pair(xbc[:, lanes]), _block_diag_pair(sf[:, lanes]),
                               _block_diag_pair(sb[:, lanes])], axis=0)
        ys.append(_dot(lhs, rhs))
    y = jnp.concatenate(ys, axis=1) + dskip_ref[...] * xbc[:, :D_SSM].astype(F32)
    y = y * _silu(z_ref[...].astype(F32))
    half = D_SSM // N_GROUPS
    normed = jnp.concatenate([_rms(y[:, g * half:(g + 1) * half]) for g in range(N_GROUPS)], axis=1)
    o_ref[...] = (normed * nw_ref[...]).astype(BF16)


def _ssd(xbc, dt_raw, z, sf_in, sb_in, bias, alog, dskip, norm_w):
    t = xbc.shape[0]
    row = lambda n: pl.BlockSpec((CHUNK, n), lambda i: (i, 0))
    st = pl.BlockSpec((1, D_STATE, D_SSM), lambda i: (i, 0, 0))
    return pl.pallas_call(
        _ssd_kernel,
        out_shape=jax.ShapeDtypeStruct((t, D_SSM), BF16),
        grid=(t // CHUNK,),
        in_specs=[row(D_XBC), row(LANES), row(D_SSM), st, st,
                  _const_spec((1, LANES)), _const_spec((1, LANES)),
                  _const_spec((1, D_SSM)), _const_spec((1, D_SSM))],
        out_specs=row(D_SSM),
        compiler_params=_params("parallel"),
        name="ssd",
    )(xbc, dt_raw, z, sf_in, sb_in, bias, alog, dskip, norm_w)


def _window_bounds(n, w):
    pos = np.arange(n)
    return pos, np.clip(pos - w // 2, 0, n), np.clip(pos + w // 2, 0, n)


def _pool_tables(rows):
    n = len(POOL_WINDOWS)
    band = np.zeros((n, LANES, LANES), np.float32)
    inv_c = np.zeros((n, LANES, POOL_GROUP), np.float32)
    inv_r = np.zeros((n, rows, POOL_GROUP), np.float32)
    for gi, w in enumerate(POOL_WINDOWS):
        pos, lo, hi = _window_bounds(GRID_W, w)
        blk = ((pos[None, :] >= lo[:, None]) & (pos[None, :] < hi[:, None])).astype(np.float32)
        band[gi, :GRID_W, :GRID_W] = blk
        band[gi, GRID_W:, GRID_W:] = blk
        inv_c[gi] = np.tile(1.0 / (hi - lo), LANES // GRID_W)[:, None]
        _, lo, hi = _window_bounds(rows, w)
        inv_r[gi] = (1.0 / (hi - lo))[:, None]
    return band, inv_c, inv_r


def _pool_kernel(u_ref, band_ref, invc_ref, invr_ref, o_ref, p_ref, q_ref, *, seq_len):
    g = pl.program_id(1)
    n_blk = seq_len // LANES
    rows = seq_len // GRID_W
    zeros = jnp.zeros((POOL_PAD, POOL_GROUP), F32)
    for buf in (p_ref, q_ref):
        buf[0:POOL_PAD, :] = zeros
        buf[POOL_PAD + seq_len:2 * POOL_PAD + seq_len, :] = zeros

    def col_body(k, carry):
        off = pl.multiple_of(k * LANES, LANES)
        m = _dot(band_ref[0], u_ref[pl.ds(off, LANES), :])
        p_ref[pl.ds(POOL_PAD + off, LANES), :] = m * invc_ref[0]
        return carry

    lax.fori_loop(0, n_blk, col_body, 0)

    def finish(src_ref, w):
        def body(r, carry):
            off = pl.multiple_of(r * GRID_W, GRID_W)
            s = src_ref[pl.ds(POOL_PAD - (w // 2) * GRID_W + off, GRID_W), :]
            mean = s * invr_ref[0, pl.ds(r, 1), :]
            o_ref[pl.ds(off, GRID_W), :] = (mean - u_ref[pl.ds(off, GRID_W), :].astype(F32)).astype(BF16)
            return carry
        lax.fori_loop(0, rows, body, 0)

    def double(src_ref, dst_ref, step):
        span = seq_len + 2 * POOL_PAD - step * GRID_W
        def body(r, carry):
            off = pl.multiple_of(r * GRID_W, GRID_W)
            dst_ref[pl.ds(off, GRID_W), :] = (src_ref[pl.ds(off, GRID_W), :]
                                              + src_ref[pl.ds(off + step * GRID_W, GRID_W), :])
            return carry
        lax.fori_loop(0, span // GRID_W, body, 0)
        dst_ref[span:seq_len + 2 * POOL_PAD, :] = jnp.zeros((step * GRID_W, POOL_GROUP), F32)

    for gi, w in enumerate(POOL_WINDOWS):
        @pl.when(g == gi)
        def _(w=w):
            src, dst = p_ref, q_ref
            step = 1
            while step < w:
                double(src, dst, step)
                src, dst = dst, src
                step *= 2
            finish(src, w)


def _pool(u, *, batch, seq_len):
    rows = seq_len // GRID_W
    band, inv_c, inv_r = _pool_tables(rows)
    n = len(POOL_WINDOWS)
    return pl.pallas_call(
        functools.partial(_pool_kernel, seq_len=seq_len),
        out_shape=jax.ShapeDtypeStruct(u.shape, BF16),
        grid=(batch, n),
        in_specs=[pl.BlockSpec((seq_len, POOL_GROUP), lambda b, g: (b, g)),
                  pl.BlockSpec((1, LANES, LANES), lambda b, g: (g, 0, 0)),
                  pl.BlockSpec((1, LANES, POOL_GROUP), lambda b, g: (g, 0, 0)),
                  pl.BlockSpec((1, rows, POOL_GROUP), lambda b, g: (g, 0, 0))],
        out_specs=pl.BlockSpec((seq_len, POOL_GROUP), lambda b, g: (b, g)),
        scratch_shapes=[pltpu.VMEM((seq_len + 2 * POOL_PAD, POOL_GROUP), F32),
                        pltpu.VMEM((seq_len + 2 * POOL_PAD, POOL_GROUP), F32)],
        compiler_params=_params("parallel", "parallel"),
        name="pool",
    )(u, jnp.asarray(band, BF16), jnp.asarray(inv_c), jnp.asarray(inv_r))


def _outproj_kernel(y_ref, d_ref, x_ref, g1_ref, pw_ref, ps_ref, wo_ref, pg_ref, o_ref):
    d = d_ref[...]
    pm = jnp.concatenate(
        [_dot(d[:, g * POOL_GROUP:(g + 1) * POOL_GROUP], pw_ref[g]) for g in range(len(POOL_WINDOWS))],
        axis=1)
    pm = (pm * ps_ref[...]).astype(BF16)
    mix = _dot(y_ref[...], wo_ref[0:D_SSM, :]) + _dot(pm, wo_ref[D_SSM:D_SSM + D_POOL, :])
    o_ref[...] = x_ref[...] + g1_ref[0] * (_rms(mix) * pg_ref[...])


def _out_proj(y, d, x2d, g1, pool_w, pool_scale, w_out, post_g, *, rows_per_mod, tm):
    t = x2d.shape[0]
    tiles_per_mod = rows_per_mod // tm
    row = lambda n: pl.BlockSpec((tm, n), lambda i: (i, 0))
    return pl.pallas_call(
        _outproj_kernel,
        out_shape=jax.ShapeDtypeStruct(x2d.shape, F32),
        grid=(t // tm,),
        in_specs=[row(D_SSM), row(D_POOL), row(D_MODEL),
                  pl.BlockSpec((1, 1, D_MODEL), lambda i: (i // tiles_per_mod, 0, 0)),
                  _const_spec(pool_w.shape), _const_spec((1, D_POOL)),
                  _const_spec(w_out.shape), _const_spec((1, D_MODEL))],
        out_specs=row(D_MODEL),
        compiler_params=_params("parallel"),
        name="out_proj",
    )(y, d, x2d, g1, pool_w, pool_scale, w_out, post_g)


def _mlp_kernel(x_ref, sh_ref, sc_ref, g2_ref, pre_ref, post_ref, w1_ref, w2_ref, o_ref, *, ff_chunk):
    x = x_ref[...]
    h = _rms(x) * pre_ref[...]
    hb = (h * (1.0 + sc_ref[0]) + sh_ref[0]).astype(BF16)
    acc = None
    for k in range(0, D_FF, ff_chunk):
        a = jnp.maximum(_dot(hb, w1_ref[:, k:k + ff_chunk]), 0.0)
        part = _dot((a * a).astype(BF16), w2_ref[k:k + ff_chunk, :])
        acc = part if acc is None else acc + part
    o_ref[...] = x + g2_ref[0] * (_rms(acc) * post_ref[...])


def _mlp(x2d, sh2, sc2, g2, pre_g, post_g, w1, w2, *, rows_per_mod, tm):
    t = x2d.shape[0]
    tiles_per_mod = rows_per_mod // tm
    row = pl.BlockSpec((tm, D_MODEL), lambda i: (i, 0))
    mod = pl.BlockSpec((1, 1, D_MODEL), lambda i: (i // tiles_per_mod, 0, 0))
    return pl.pallas_call(
        functools.partial(_mlp_kernel, ff_chunk=1024),
        out_shape=jax.ShapeDtypeStruct(x2d.shape, F32),
        grid=(t // tm,),
        in_specs=[row, mod, mod, mod, _const_spec((1, D_MODEL)), _const_spec((1, D_MODEL)),
                  _const_spec(w1.shape), _const_spec(w2.shape)],
        out_specs=row,
        compiler_params=_params("parallel"),
        name="mlp",
    )(x2d, sh2, sc2, g2, pre_g, post_g, w1, w2)


def _pad_lanes(v):
    return jnp.pad(v.reshape(1, -1).astype(F32), ((0, 0), (0, LANES - v.size)))


def kernel(x, c, ctx, c_ctx, w_ada, b_ada, pre_mix_g, post_mix_g, pre_mlp_g, post_mlp_g, w_in, conv_w,
           conv_b, dt_bias, a_log, d_skip, ssm_norm_w, pool_w, pool_scale, w_out, w_mlp1, w_mlp2):
    batch, seq_len, d = x.shape
    ctx_len = ctx.shape[1]
    assert w_ada.shape[0] == 1, "single layer"
    row = lambda v: v.reshape(1, -1).astype(F32)

    n_rows = -(-(batch + 1) // 8) * 8
    cc = jnp.zeros((n_rows, d), F32).at[:batch].set(c.astype(F32)).at[batch].set(c_ctx.astype(F32))
    ada = _ada(cc, w_ada[0].astype(F32), row(b_ada[0]))
    sh1, sc1, g1, sh2, sc2, g2 = [ada[:batch, k * d:(k + 1) * d].reshape(batch, 1, d) for k in range(6)]
    csh1 = ada[batch:batch + 1, 0:d].reshape(1, 1, d)
    csc1 = ada[batch:batch + 1, d:2 * d].reshape(1, 1, d)

    wi = w_in[0]
    o_x, o_dt, o_u = D_SSM, D_SSM + D_XBC, D_SSM + D_XBC + 2 * N_HEADS
    wz = wi[:, :o_x].astype(BF16)
    wx = wi[:, o_x:o_dt].astype(BF16)
    wd = jnp.pad(wi[:, o_dt:o_u], ((0, 0), (0, LANES - 2 * N_HEADS))).astype(BF16)
    wu = wi[:, o_u:].astype(BF16)
    cw = jnp.pad(conv_w[0].astype(F32), ((0, 8 - D_CONV), (0, 0)))
    cb = row(conv_b[0])
    bias = _pad_lanes(dt_bias[0])
    alog = _pad_lanes(a_log[0])
    dskip = jnp.repeat(d_skip[0].astype(F32), HEADDIM).reshape(1, D_SSM)
    pre_g = row(pre_mix_g[0])

    nc_ctx = ctx_len // CHUNK
    ctx2d = ctx.reshape(batch * ctx_len, d).astype(F32)
    xbc_c, dt_c = _in_proj(ctx2d, csh1, csc1, pre_g, wx, wd, rows_per_mod=batch * ctx_len, tm=ctx_len)
    xbc_c = _conv(xbc_c, cw, cb, seq_len=ctx_len, tc=ctx_len)
    h0 = jnp.zeros((batch, D_STATE, D_SSM), F32)
    _, _, s_f, s_b = _states(xbc_c, dt_c, bias, alog, h0, h0, batch=batch, nc=nc_ctx)

    tm = 512
    nc = seq_len // CHUNK
    x2d = x.reshape(batch * seq_len, d).astype(F32)
    xbc, dt_raw, z, u = _in_proj(x2d, sh1, sc1, pre_g, wx, wd, wz, wu, rows_per_mod=seq_len, tm=tm)
    xbc = _conv(xbc, cw, cb, seq_len=seq_len, tc=tm)
    sf_in, sb_in, _, _ = _states(xbc, dt_raw, bias, alog, s_f, s_b, batch=batch, nc=nc)
    y = _ssd(xbc, dt_raw, z, sf_in, sb_in, bias, alog, dskip, row(ssm_norm_w[0]))
    dpool = _pool(u, batch=batch, seq_len=seq_len)
    x1 = _out_proj(y, dpool, x2d, g1, pool_w[0].astype(BF16), row(pool_scale[0]),
                   w_out[0].astype(BF16), row(post_mix_g[0]), rows_per_mod=seq_len, tm=tm)

    x2 = _mlp(x1, sh2, sc2, g2, row(pre_mlp_g[0]), row(post_mlp_g[0]),
              w_mlp1[0].astype(BF16), w_mlp2[0].astype(BF16), rows_per_mod=seq_len, tm=tm)
    return x2.reshape(batch, seq_len, d).astype(x.dtype)
```

```python
import functools

import jax
import jax.numpy as jnp
import numpy as np
from jax import lax
from jax.experimental import pallas as pl
from jax.experimental.pallas import tpu as pltpu

F32 = jnp.float32
BF16 = jnp.bfloat16

D_MODEL = 1024
GRID_W = 64
N_HEADS = 16
HEADDIM = 64
N_GROUPS = 2
D_STATE = 128
D_CONV = 5
CHUNK = 128
D_SSM = N_HEADS * HEADDIM
D_BC = N_GROUPS * D_STATE
D_XBC = D_SSM + 2 * D_BC
POOL_WINDOWS = (2, 4, 8, 16)
POOL_GROUP = 256
D_POOL = POOL_GROUP * len(POOL_WINDOWS)
D_FF = 4 * D_MODEL
EPS = 1e-6

LANES = 128
F32_SUBLANES = 8
BF16_SUBLANES = 16
VMEM_LIMIT = 56 * 1024 * 1024
HALO = BF16_SUBLANES
POOL_PAD = (max(POOL_WINDOWS) // 2) * GRID_W
TOKEN_TILE = 512
ADA_TILE = 1536
FF_TILE = 1024
POOL_UNROLL = 4


def _params(*sem):
    return pltpu.CompilerParams(dimension_semantics=sem, vmem_limit_bytes=VMEM_LIMIT)


def _dot(a, b):
    return jnp.dot(a, b, preferred_element_type=F32)


def _split(a):
    hi = a.astype(BF16)
    lo = (a - hi.astype(F32)).astype(BF16)
    return hi, lo


def _silu(v):
    return v * (1.0 / (1.0 + jnp.exp(-v)))


def _softplus(v):
    return jnp.maximum(v, 0.0) + jnp.log(1.0 + jnp.exp(-jnp.abs(v)))


def _rms(v):
    return v * lax.rsqrt(jnp.mean(v * v, axis=-1, keepdims=True) + EPS)


def _const_spec(shape):
    nd = len(shape)
    return pl.BlockSpec(shape, lambda *_: (0,) * nd)


def _ada_kernel(c_ref, w_ref, b_ref, o_ref):
    s = _silu(c_ref[...])
    o_ref[...] = jnp.dot(s, w_ref[...], precision=lax.Precision.HIGHEST,
                         preferred_element_type=F32) + b_ref[...]


def _ada(cc, w, b):
    rows, d = cc.shape
    n = w.shape[1]
    tn = ADA_TILE
    return pl.pallas_call(
        _ada_kernel,
        out_shape=jax.ShapeDtypeStruct((rows, n), F32),
        grid=(n // tn,),
        in_specs=[_const_spec((rows, d)),
                  pl.BlockSpec((d, tn), lambda j: (0, j)),
                  pl.BlockSpec((1, tn), lambda j: (0, j))],
        out_specs=pl.BlockSpec((rows, tn), lambda j: (0, j)),
        compiler_params=_params("parallel"),
        name="ada",
    )(cc, w, b)


def _inproj_kernel(x_ref, sh_ref, sc_ref, g_ref, wx_ref, wd_ref, *rest, with_zu):
    h = _rms(x_ref[...]) * g_ref[...]
    h = h * (1.0 + sc_ref[0]) + sh_ref[0]
    hb = h.astype(BF16)
    if with_zu:
        wz_ref, wu_ref, xbc_ref, dt_ref, z_ref, u_ref = rest
        z_ref[...] = _dot(hb, wz_ref[...]).astype(BF16)
        u_ref[...] = _dot(hb, wu_ref[...]).astype(BF16)
    else:
        xbc_ref, dt_ref = rest
    xbc_ref[...] = _dot(hb, wx_ref[...]).astype(BF16)
    dt_ref[...] = _dot(hb, wd_ref[...])


def _in_proj(x2d, shift, scale, gain, wx, wd, wz=None, wu=None, *, rows_per_mod, tm):
    t, d = x2d.shape
    with_zu = wz is not None
    tiles_per_mod = rows_per_mod // tm
    mod_spec = pl.BlockSpec((1, 1, d), lambda i: (i // tiles_per_mod, 0, 0))
    row = lambda n: pl.BlockSpec((tm, n), lambda i: (i, 0))
    in_specs = [row(d), mod_spec, mod_spec, _const_spec((1, d)),
                _const_spec(wx.shape), _const_spec(wd.shape)]
    args = [x2d, shift, scale, gain, wx, wd]
    out_shape = [jax.ShapeDtypeStruct((t, D_XBC), BF16), jax.ShapeDtypeStruct((t, LANES), F32)]
    out_specs = [row(D_XBC), row(LANES)]
    if with_zu:
        in_specs += [_const_spec(wz.shape), _const_spec(wu.shape)]
        args += [wz, wu]
        out_shape += [jax.ShapeDtypeStruct((t, D_SSM), BF16), jax.ShapeDtypeStruct((t, D_POOL), BF16)]
        out_specs += [row(D_SSM), row(D_POOL)]
    return pl.pallas_call(
        functools.partial(_inproj_kernel, with_zu=with_zu),
        out_shape=out_shape,
        grid=(t // tm,),
        in_specs=in_specs,
        out_specs=out_specs,
        compiler_params=_params("parallel"),
        name="in_proj" if with_zu else "in_proj_ctx",
    )(*args)


def _conv_kernel(prev_ref, main_ref, next_ref, w_ref, b_ref, o_ref, ext_ref, *, tiles_per_seq, tc):
    i = pl.program_id(0)
    pos = i % tiles_per_seq
    keep_prev = jnp.where(pos == 0, 0.0, 1.0)
    keep_next = jnp.where(pos == tiles_per_seq - 1, 0.0, 1.0)
    ext_ref[0:HALO, :] = prev_ref[...].astype(F32) * keep_prev
    ext_ref[HALO:HALO + tc, :] = main_ref[...].astype(F32)
    ext_ref[HALO + tc:2 * HALO + tc, :] = next_ref[...].astype(F32) * keep_next
    acc = b_ref[...] + w_ref[0:1, :] * ext_ref[HALO - 2:HALO - 2 + tc, :]
    for k in range(1, D_CONV):
        acc = acc + w_ref[k:k + 1, :] * ext_ref[HALO - 2 + k:HALO - 2 + k + tc, :]
    o_ref[...] = _silu(acc).astype(BF16)


def _conv(xbc_raw, conv_w, conv_b, *, seq_len, tc):
    t = xbc_raw.shape[0]
    tiles_per_seq = seq_len // tc
    halo_blocks = tc // HALO
    last_halo = t // HALO - 1
    return pl.pallas_call(
        functools.partial(_conv_kernel, tiles_per_seq=tiles_per_seq, tc=tc),
        out_shape=jax.ShapeDtypeStruct((t, D_XBC), BF16),
        grid=(t // tc,),
        in_specs=[pl.BlockSpec((HALO, D_XBC), lambda i: (jnp.maximum(i * halo_blocks - 1, 0), 0)),
                  pl.BlockSpec((tc, D_XBC), lambda i: (i, 0)),
                  pl.BlockSpec((HALO, D_XBC), lambda i: (jnp.minimum((i + 1) * halo_blocks, last_halo), 0)),
                  _const_spec(conv_w.shape), _const_spec(conv_b.shape)],
        out_specs=pl.BlockSpec((tc, D_XBC), lambda i: (i, 0)),
        scratch_shapes=[pltpu.VMEM((tc + 2 * HALO, D_XBC), F32)],
        compiler_params=_params("parallel"),
        name="conv",
    )(xbc_raw, xbc_raw, xbc_raw, conv_w, conv_b)


def _tri(lower):
    r = lax.broadcasted_iota(jnp.int32, (CHUNK, CHUNK), 0)
    c = lax.broadcasted_iota(jnp.int32, (CHUNK, CHUNK), 1)
    return (r >= c) if lower else (r <= c)


def _head_expander(offset):
    r = lax.broadcasted_iota(jnp.int32, (LANES, D_SSM), 0)
    c = lax.broadcasted_iota(jnp.int32, (LANES, D_SSM), 1)
    return jnp.where(r == c // HEADDIM + offset, 1.0, 0.0).astype(BF16)


def _block_diag_pair(v):
    lane = lax.broadcasted_iota(jnp.int32, v.shape, 1)
    zero = jnp.zeros_like(v)
    return jnp.concatenate([jnp.where(lane < HEADDIM, v, zero), jnp.where(lane >= HEADDIM, v, zero)], axis=0)


def _scan_direction(xbc, raw, bias, a, s_ref, y_ref, dskip, *, forward):
    lane0 = 0 if forward else N_HEADS
    edge = CHUNK - 1 if forward else 0
    mask = _tri(forward)
    tri = jnp.where(mask, 1.0, 0.0).astype(BF16)

    dt = _softplus(raw + bias)
    hi, lo = _split(dt * a)
    acs = _dot(tri, hi) + _dot(tri, lo)
    lane = lax.broadcasted_iota(jnp.int32, (CHUNK, LANES), 1)
    rows = jnp.where(lane < 2 * N_HEADS, acs, dt).T
    a_row = rows[0:2 * N_HEADS]
    g_row = a_row - jnp.log(rows[2 * N_HEADS:4 * N_HEADS])
    w_row = jnp.exp(a_row[:, edge:edge + 1] - g_row)
    end = jnp.broadcast_to(jnp.exp(acs[edge:edge + 1, :]), (F32_SUBLANES, LANES))
    e_hi, e_lo = _split(end)
    expander = _head_expander(lane0)
    dec = (_dot(e_hi, expander) + _dot(e_lo, expander))[0:1, :]

    s_in = s_ref[...]
    half = D_SSM // N_GROUPS
    bts, cbs, yoffs = [], [], []
    for g in range(N_GROUPS):
        b_g = xbc[:, D_SSM + g * D_STATE:D_SSM + (g + 1) * D_STATE]
        bts.append(b_g.astype(F32).T)
        if y_ref is not None:
            c_g = xbc[:, D_SSM + D_BC + g * D_STATE:D_SSM + D_BC + (g + 1) * D_STATE]
            cbs.append(lax.dot_general(c_g, b_g, (((1,), (1,)), ((), ())), preferred_element_type=F32))
            yoffs.append(_dot(c_g, s_in[:, g * half:(g + 1) * half].astype(BF16)))

    lane_sq = lax.broadcasted_iota(jnp.int32, (CHUNK, CHUNK), 1)
    for pair in range(N_HEADS // 2):
        g = (2 * pair) // (N_HEADS // N_GROUPS)
        lanes = slice(pair * LANES, (pair + 1) * LANES)
        m_parts, w_parts, ea = [], [], []
        for h in (2 * pair, 2 * pair + 1):
            hl = lane0 + h
            if y_ref is not None:
                a_c = jnp.broadcast_to(acs[:, hl:hl + 1], (CHUNK, CHUNK))
                seg = jnp.where(mask, a_c - g_row[hl:hl + 1, :], -jnp.inf)
                m_parts.append((cbs[g] * jnp.exp(seg)).astype(BF16))
                ea.append(jnp.exp(a_c))
            w_parts.append((bts[g] * w_row[hl:hl + 1, :]).astype(BF16))
        xs_bd = _block_diag_pair(xbc[:, lanes])
        lhs = jnp.concatenate(w_parts, axis=1)
        if y_ref is not None:
            lhs = jnp.concatenate([jnp.concatenate(m_parts, axis=1), lhs], axis=0)
        res = _dot(lhs, xs_bd)
        if y_ref is not None:
            local = slice((pair * LANES) % half, (pair * LANES) % half + LANES)
            y = res[0:CHUNK] + yoffs[g][:, local] * jnp.where(lane_sq < HEADDIM, ea[0], ea[1])
            if dskip is not None:
                y = y + dskip[:, lanes] * xbc[:, lanes].astype(F32)
            y_ref[:, lanes] = y.astype(BF16)
            res = res[CHUNK:2 * CHUNK]
        s_ref[:, lanes] = s_in[:, lanes] * dec[:, lanes] + res


def _scan_kernel(xf_ref, df_ref, xb_ref, db_ref, bias_ref, alog_ref, dskip_ref, h0f_ref, h0b_ref,
                 of_ref, ob_ref, sf_ref, sb_ref, *, nc, emit_y):
    j = pl.program_id(1)

    @pl.when(j == 0)
    def _():
        sf_ref[...] = h0f_ref[0]
        sb_ref[...] = h0b_ref[0]

    a = -jnp.exp(alog_ref[...])
    bias = bias_ref[...]
    _scan_direction(xf_ref[...], df_ref[...], bias, a, sf_ref, of_ref if emit_y else None,
                    dskip_ref[...], forward=True)
    _scan_direction(xb_ref[...], db_ref[...], bias, a, sb_ref, ob_ref if emit_y else None,
                    None, forward=False)

    if not emit_y:
        @pl.when(j == nc - 1)
        def _():
            of_ref[0] = sf_ref[...]
            ob_ref[0] = sb_ref[...]


def _scan(xbc, dt_raw, bias, alog, dskip, h0f, h0b, *, batch, nc, emit_y):
    t = xbc.shape[0]
    fwd = lambda n: pl.BlockSpec((CHUNK, n), lambda b, j: (b * nc + j, 0))
    bwd = lambda n: pl.BlockSpec((CHUNK, n), lambda b, j: (b * nc + nc - 1 - j, 0))
    st_b = pl.BlockSpec((1, D_STATE, D_SSM), lambda b, j: (b, 0, 0))
    if emit_y:
        out_shape = [jax.ShapeDtypeStruct((t, D_SSM), BF16)] * 2
        out_specs = [fwd(D_SSM), bwd(D_SSM)]
    else:
        out_shape = [jax.ShapeDtypeStruct((batch, D_STATE, D_SSM), F32)] * 2
        out_specs = [st_b, st_b]
    return pl.pallas_call(
        functools.partial(_scan_kernel, nc=nc, emit_y=emit_y),
        out_shape=out_shape,
        grid=(batch, nc),
        in_specs=[fwd(D_XBC), fwd(LANES), bwd(D_XBC), bwd(LANES),
                  _const_spec((1, LANES)), _const_spec((1, LANES)), _const_spec((1, D_SSM)), st_b, st_b],
        out_specs=out_specs,
        scratch_shapes=[pltpu.VMEM((D_STATE, D_SSM), F32), pltpu.VMEM((D_STATE, D_SSM), F32)],
        compiler_params=_params("parallel", "arbitrary"),
        name="scan" if emit_y else "scan_ctx",
    )(xbc, dt_raw, xbc, dt_raw, bias, alog, dskip, h0f, h0b)


def _window_bounds(n, w):
    pos = np.arange(n)
    return pos, np.clip(pos - w // 2, 0, n), np.clip(pos + w // 2, 0, n)


def _pool_tables(rows):
    n = len(POOL_WINDOWS)
    band = np.zeros((n, LANES, LANES), np.float32)
    inv_c = np.zeros((n, LANES, POOL_GROUP), np.float32)
    inv_r = np.zeros((n, rows, POOL_GROUP), np.float32)
    for gi, w in enumerate(POOL_WINDOWS):
        pos, lo, hi = _window_bounds(GRID_W, w)
        blk = ((pos[None, :] >= lo[:, None]) & (pos[None, :] < hi[:, None])).astype(np.float32)
        band[gi, :GRID_W, :GRID_W] = blk
        band[gi, GRID_W:, GRID_W:] = blk
        inv_c[gi] = np.tile(1.0 / (hi - lo), LANES // GRID_W)[:, None]
        _, lo, hi = _window_bounds(rows, w)
        inv_r[gi] = (1.0 / (hi - lo))[:, None]
    return band, inv_c, inv_r


def _pool_kernel(u_ref, band_ref, invc_ref, invr_ref, o_ref, p_ref, q_ref, *, seq_len):
    g = pl.program_id(1)
    n_blk = seq_len // LANES
    rows = seq_len // GRID_W
    zeros = jnp.zeros((POOL_PAD, POOL_GROUP), F32)
    for buf in (p_ref, q_ref):
        buf[0:POOL_PAD, :] = zeros
        buf[POOL_PAD + seq_len:2 * POOL_PAD + seq_len, :] = zeros

    def col_body(k, carry):
        for i in range(POOL_UNROLL):
            off = pl.multiple_of((k * POOL_UNROLL + i) * LANES, LANES)
            m = _dot(band_ref[0], u_ref[pl.ds(off, LANES), :])
            p_ref[pl.ds(POOL_PAD + off, LANES), :] = m * invc_ref[0]
        return carry

    lax.fori_loop(0, n_blk // POOL_UNROLL, col_body, 0)

    def finish(src_ref, w):
        def body(r, carry):
            off = pl.multiple_of(r * GRID_W, GRID_W)
            s = src_ref[pl.ds(POOL_PAD - (w // 2) * GRID_W + off, GRID_W), :]
            mean = s * invr_ref[0, pl.ds(r, 1), :]
            o_ref[pl.ds(off, GRID_W), :] = (mean - u_ref[pl.ds(off, GRID_W), :].astype(F32)).astype(BF16)
            return carry
        lax.fori_loop(0, rows, body, 0, unroll=2)

    def double(src_ref, dst_ref, step):
        span = seq_len + 2 * POOL_PAD - step * GRID_W
        def body(r, carry):
            off = pl.multiple_of(r * GRID_W, GRID_W)
            dst_ref[pl.ds(off, GRID_W), :] = (src_ref[pl.ds(off, GRID_W), :]
                                              + src_ref[pl.ds(off + step * GRID_W, GRID_W), :])
            return carry
        lax.fori_loop(0, span // GRID_W, body, 0, unroll=2)
        dst_ref[span:seq_len + 2 * POOL_PAD, :] = jnp.zeros((step * GRID_W, POOL_GROUP), F32)

    for gi, w in enumerate(POOL_WINDOWS):
        @pl.when(g == gi)
        def _(w=w):
            src, dst = p_ref, q_ref
            step = 1
            while step < w:
                double(src, dst, step)
                src, dst = dst, src
                step *= 2
            finish(src, w)


def _pool(u, *, batch, seq_len):
    rows = seq_len // GRID_W
    band, inv_c, inv_r = _pool_tables(rows)
    n = len(POOL_WINDOWS)
    return pl.pallas_call(
        functools.partial(_pool_kernel, seq_len=seq_len),
        out_shape=jax.ShapeDtypeStruct(u.shape, BF16),
        grid=(batch, n),
        in_specs=[pl.BlockSpec((seq_len, POOL_GROUP), lambda b, g: (b, g)),
                  pl.BlockSpec((1, LANES, LANES), lambda b, g: (g, 0, 0)),
                  pl.BlockSpec((1, LANES, POOL_GROUP), lambda b, g: (g, 0, 0)),
                  pl.BlockSpec((1, rows, POOL_GROUP), lambda b, g: (g, 0, 0))],
        out_specs=pl.BlockSpec((seq_len, POOL_GROUP), lambda b, g: (b, g)),
        scratch_shapes=[pltpu.VMEM((seq_len + 2 * POOL_PAD, POOL_GROUP), F32),
                        pltpu.VMEM((seq_len + 2 * POOL_PAD, POOL_GROUP), F32)],
        compiler_params=_params("parallel", "parallel"),
        name="pool",
    )(u, jnp.asarray(band, BF16), jnp.asarray(inv_c), jnp.asarray(inv_r))


def _outproj_kernel(yf_ref, yb_ref, z_ref, d_ref, x_ref, g1_ref, nw_ref, pw_ref, ps_ref, wo_ref, pg_ref, o_ref):
    y = (yf_ref[...].astype(F32) + yb_ref[...].astype(F32)) * _silu(z_ref[...].astype(F32))
    half = D_SSM // N_GROUPS
    y = jnp.concatenate([_rms(y[:, g * half:(g + 1) * half]) for g in range(N_GROUPS)], axis=1)
    yn = (y * nw_ref[...]).astype(BF16)
    d = d_ref[...]
    pm = jnp.concatenate(
        [_dot(d[:, g * POOL_GROUP:(g + 1) * POOL_GROUP], pw_ref[g]) for g in range(len(POOL_WINDOWS))],
        axis=1)
    pm = (pm * ps_ref[...]).astype(BF16)
    mix = _dot(yn, wo_ref[0:D_SSM, :]) + _dot(pm, wo_ref[D_SSM:D_SSM + D_POOL, :])
    o_ref[...] = x_ref[...] + g1_ref[0] * (_rms(mix) * pg_ref[...])


def _out_proj(yf, yb, z, d, x2d, g1, norm_w, pool_w, pool_scale, w_out, post_g, *, rows_per_mod, tm):
    t = x2d.shape[0]
    tiles_per_mod = rows_per_mod // tm
    row = lambda n: pl.BlockSpec((tm, n), lambda i: (i, 0))
    return pl.pallas_call(
        _outproj_kernel,
        out_shape=jax.ShapeDtypeStruct(x2d.shape, F32),
        grid=(t // tm,),
        in_specs=[row(D_SSM), row(D_SSM), row(D_SSM), row(D_POOL), row(D_MODEL),
                  pl.BlockSpec((1, 1, D_MODEL), lambda i: (i // tiles_per_mod, 0, 0)),
                  _const_spec((1, D_SSM)), _const_spec(pool_w.shape), _const_spec((1, D_POOL)),
                  _const_spec(w_out.shape), _const_spec((1, D_MODEL))],
        out_specs=row(D_MODEL),
        compiler_params=_params("parallel"),
        name="out_proj",
    )(yf, yb, z, d, x2d, g1, norm_w, pool_w, pool_scale, w_out, post_g)


def _mlp_kernel(x_ref, sh_ref, sc_ref, g2_ref, pre_ref, post_ref, w1_ref, w2_ref, o_ref, *, ff_chunk):
    x = x_ref[...]
    h = _rms(x) * pre_ref[...]
    hb = (h * (1.0 + sc_ref[0]) + sh_ref[0]).astype(BF16)
    acc = None
    for k in range(0, D_FF, ff_chunk):
        a = jnp.maximum(_dot(hb, w1_ref[:, k:k + ff_chunk]), 0.0)
        part = _dot((a * a).astype(BF16), w2_ref[k:k + ff_chunk, :])
        acc = part if acc is None else acc + part
    o_ref[...] = x + g2_ref[0] * (_rms(acc) * post_ref[...])


def _mlp(x2d, sh2, sc2, g2, pre_g, post_g, w1, w2, *, rows_per_mod, tm):
    t = x2d.shape[0]
    tiles_per_mod = rows_per_mod // tm
    row = pl.BlockSpec((tm, D_MODEL), lambda i: (i, 0))
    mod = pl.BlockSpec((1, 1, D_MODEL), lambda i: (i // tiles_per_mod, 0, 0))
    return pl.pallas_call(
        functools.partial(_mlp_kernel, ff_chunk=FF_TILE),
        out_shape=jax.ShapeDtypeStruct(x2d.shape, F32),
        grid=(t // tm,),
        in_specs=[row, mod, mod, mod, _const_spec((1, D_MODEL)), _const_spec((1, D_MODEL)),
                  _const_spec(w1.shape), _const_spec(w2.shape)],
        out_specs=row,
        compiler_params=_params("parallel"),
        name="mlp",
    )(x2d, sh2, sc2, g2, pre_g, post_g, w1, w2)


def _dt_lanes(v, axis):
    pad = [(0, 0)] * v.ndim
    pad[axis] = (0, LANES - 4 * N_HEADS)
    return jnp.pad(jnp.concatenate([v, v], axis=axis), pad)


def kernel(x, c, ctx, c_ctx, w_ada, b_ada, pre_mix_g, post_mix_g, pre_mlp_g, post_mlp_g, w_in, conv_w,
           conv_b, dt_bias, a_log, d_skip, ssm_norm_w, pool_w, pool_scale, w_out, w_mlp1, w_mlp2):
    batch, seq_len, d = x.shape
    ctx_len = ctx.shape[1]
    assert w_ada.shape[0] == 1, "single layer"
    row = lambda v: v.reshape(1, -1).astype(F32)

    n_rows = -(-(batch + 1) // F32_SUBLANES) * F32_SUBLANES
    cc = jnp.zeros((n_rows, d), F32).at[:batch].set(c.astype(F32)).at[batch].set(c_ctx.astype(F32))
    ada = _ada(cc, w_ada[0].astype(F32), row(b_ada[0]))
    sh1, sc1, g1, sh2, sc2, g2 = [ada[:batch, k * d:(k + 1) * d].reshape(batch, 1, d) for k in range(6)]
    csh1 = ada[batch:batch + 1, 0:d].reshape(1, 1, d)
    csc1 = ada[batch:batch + 1, d:2 * d].reshape(1, 1, d)

    wi = w_in[0]
    o_x, o_dt, o_u = D_SSM, D_SSM + D_XBC, D_SSM + D_XBC + 2 * N_HEADS
    wz = wi[:, :o_x].astype(BF16)
    wx = wi[:, o_x:o_dt].astype(BF16)
    wd = _dt_lanes(wi[:, o_dt:o_u], 1).astype(BF16)
    wu = wi[:, o_u:].astype(BF16)
    cw = jnp.pad(conv_w[0].astype(F32), ((0, F32_SUBLANES - D_CONV), (0, 0)))
    cb = row(conv_b[0])
    bias = _dt_lanes(dt_bias[0].reshape(1, -1).astype(F32), 1)
    alog = _dt_lanes(a_log[0].reshape(1, -1).astype(F32), 1)
    dskip = jnp.repeat(d_skip[0].astype(F32), HEADDIM).reshape(1, D_SSM)
    pre_g = row(pre_mix_g[0])

    nc_ctx = ctx_len // CHUNK
    ctx2d = ctx.reshape(batch * ctx_len, d).astype(F32)
    xbc_c, dt_c = _in_proj(ctx2d, csh1, csc1, pre_g, wx, wd, rows_per_mod=batch * ctx_len, tm=ctx_len)
    xbc_c = _conv(xbc_c, cw, cb, seq_len=ctx_len, tc=ctx_len)
    h0 = jnp.zeros((batch, D_STATE, D_SSM), F32)
    s_f, s_b = _scan(xbc_c, dt_c, bias, alog, dskip, h0, h0, batch=batch, nc=nc_ctx, emit_y=False)

    tm = TOKEN_TILE
    nc = seq_len // CHUNK
    x2d = x.reshape(batch * seq_len, d).astype(F32)
    xbc, dt_raw, z, u = _in_proj(x2d, sh1, sc1, pre_g, wx, wd, wz, wu, rows_per_mod=seq_len, tm=tm)
    xbc = _conv(xbc, cw, cb, seq_len=seq_len, tc=tm)
    yf, yb = _scan(xbc, dt_raw, bias, alog, dskip, s_f, s_b, batch=batch, nc=nc, emit_y=True)
    dpool = _pool(u, batch=batch, seq_len=seq_len)
    x1 = _out_proj(yf, yb, z, dpool, x2d, g1, row(ssm_norm_w[0]), pool_w[0].astype(BF16), row(pool_scale[0]),
                   w_out[0].astype(BF16), row(post_mix_g[0]), rows_per_mod=seq_len, tm=tm)

    x2 = _mlp(x1, sh2, sc2, g2, row(pre_mlp_g[0]), row(post_mlp_g[0]),
              w_mlp1[0].astype(BF16), w_mlp2[0].astype(BF16), rows_per_mod=seq_len, tm=tm)
    return x2.reshape(batch, seq_len, d).astype(x.dtype)
```
